```python
import math
import jax, jax.numpy as jnp
from jax import lax
import numpy as np

D_MODEL = 1024
BATCH = 8
SEQ = 4096
DEPTH = 4

CHUNK = 64
MIX_W = 3 * D_MODEL // 4
N_BRANCH = 3
CONV_K = 4
HG_DK = 128
HG_DV = 128
HG_HEADS = MIX_W // HG_DV
F_MIN = 1e-30
SSM_HEADDIM = 64
SSM_HEADS = MIX_W // SSM_HEADDIM
SSM_STATE = 128
SSM_GROUPS = 2
SSM_CONV_CH = MIX_W + 2 * SSM_GROUPS * SSM_STATE
GDN_DK = 128
GDN_DV = 128
GDN_HEADS = MIX_W // GDN_DV
GDN_QKV = GDN_HEADS * (2 * GDN_DK + GDN_DV)
FFN_HIDDEN = (((8 * D_MODEL + 2) // 3 + 255) // 256) * 256
NORM_EPS = 1e-6

IN_SIZES = (
    HG_HEADS * HG_DK,
    HG_HEADS * HG_DK,
    HG_HEADS * HG_DV,
    HG_HEADS * HG_DV,
    MIX_W,
    SSM_CONV_CH,
    SSM_HEADS,
    GDN_QKV,
    GDN_HEADS * GDN_DV,
    GDN_HEADS,
    GDN_HEADS,
    N_BRANCH * D_MODEL,
)
IN_WIDTH = sum(IN_SIZES)
IN_SPLITS = tuple(int(v) for v in np.cumsum(IN_SIZES)[:-1])

kernel_name = 'hybrid_hgrn2_ssd_gdn_streaming_trunk'


def rms_norm(x, w):
    xf = x.astype(jnp.float32)
    y = xf * lax.rsqrt(jnp.mean(xf * xf, axis=-1, keepdims=True) + NORM_EPS)
    return y.astype(x.dtype) * w


def l2_normalize(x):
    xf = x.astype(jnp.float32)
    return (xf * lax.rsqrt(jnp.sum(xf * xf, axis=-1, keepdims=True) + NORM_EPS)).astype(x.dtype)


def causal_depthwise_conv(x, w, b=None):
    k, ch = w.shape
    y = lax.conv_general_dilated(x, w[:, None, :], window_strides=(1,), padding=((k - 1, 0),),
                                 dimension_numbers=('NWC', 'WIO', 'NWC'), feature_group_count=ch)
    return y if b is None else y + b


def to_chunks(t):
    bsz, s, h, d = t.shape
    return t.reshape(bsz, s // CHUNK, CHUNK, h, d).transpose(1, 0, 3, 2, 4)


def from_chunks(t):
    nc, bsz, h, c, d = t.shape
    return t.transpose(1, 0, 3, 2, 4).reshape(bsz, nc * c, h, d)


def causal_mask(strict=False):
    return jnp.tril(jnp.ones((CHUNK, CHUNK), dtype=bool), -1 if strict else 0)


def masked_exp(diff, mask):
    return jnp.where(mask, jnp.exp(jnp.where(mask, diff, 0.0)), 0.0)


def gla_chunked(q, k, v, logf):
    bsz, _, heads, dk = q.shape
    dv = v.shape[-1]
    mask = causal_mask()[:, :, None]

    def step(state, inp):
        q_c, k_c, v_c, b_c = inp
        o_inter = jnp.einsum('bhtk,bhkv->bhtv', q_c * jnp.exp(b_c), state)
        pair = masked_exp(b_c[:, :, :, None, :] - b_c[:, :, None, :, :], mask)
        scores = jnp.einsum('bhtk,bhsk,bhtsk->bhts', q_c, k_c, pair)
        o = o_inter + jnp.einsum('bhts,bhsv->bhtv', scores, v_c)
        b_end = b_c[:, :, -1:, :]
        new_state = (jnp.exp(b_end[:, :, 0, :, None]) * state
                     + jnp.einsum('bhsk,bhsv->bhkv', k_c * jnp.exp(b_end - b_c), v_c))
        return new_state, o

    b = jnp.cumsum(to_chunks(logf), axis=3)
    init = jnp.zeros((bsz, heads, dk, dv), v.dtype)
    _, o = lax.scan(step, init, (to_chunks(q), to_chunks(k), to_chunks(v), b))
    return from_chunks(o)


def ssd_chunked(cm, bm, xdt, loga):
    qc, kc, vc = to_chunks(cm), to_chunks(bm), to_chunks(xdt)
    cum = jnp.cumsum(to_chunks(loga[..., None])[..., 0], axis=-1)
    seg = masked_exp(cum[..., :, None] - cum[..., None, :], causal_mask())
    y_intra = jnp.einsum('nbhts,nbhsp->nbhtp', jnp.einsum('nbhtk,nbhsk->nbhts', qc, kc) * seg, vc)
    chunk_states = jnp.einsum('nbhsk,nbhsp->nbhkp', kc * jnp.exp(cum[..., -1:] - cum)[..., None], vc)
    chunk_decay = jnp.exp(cum[..., -1])

    def carry(state, inp):
        local, dec = inp
        return dec[..., None, None] * state + local, state

    init = jnp.zeros(chunk_states.shape[1:], chunk_states.dtype)
    _, starts = lax.scan(carry, init, (chunk_states, chunk_decay))
    y_inter = jnp.einsum('nbhtk,nbhkp->nbhtp', qc * jnp.exp(cum)[..., None], starts)
    return from_chunks(y_intra + y_inter)


def gated_delta_chunked(q, k, v, logg, beta):
    bsz, _, heads, dk = q.shape
    dv = v.shape[-1]
    qc, kc, vc = to_chunks(q), to_chunks(k), to_chunks(v)
    bc = to_chunks(beta[..., None])[..., 0]
    cum = jnp.cumsum(to_chunks(logg[..., None])[..., 0], axis=-1)
    decay = masked_exp(cum[..., :, None] - cum[..., None, :], causal_mask())
    kk = jnp.einsum('nbhtk,nbhsk->nbhts', kc, kc)
    a_low = jnp.where(causal_mask(strict=True), bc[..., :, None] * kk * decay, 0.0)
    rhs = jnp.concatenate([vc * bc[..., None], kc * (bc * jnp.exp(cum))[..., None]], axis=-1)
    eye = jnp.eye(CHUNK, dtype=jnp.float32)
    sol = lax.linalg.triangular_solve(eye + a_low.astype(jnp.float32), rhs.astype(jnp.float32),
                                      left_side=True, lower=True, unit_diagonal=True).astype(v.dtype)
    u_base, w_corr = sol[..., :dv], sol[..., dv:]
    qk = jnp.einsum('nbhtk,nbhsk->nbhts', qc, kc) * decay

    def step(state, inp):
        q_c, k_c, u_b, w_c, qk_c, cum_c = inp
        u = u_b - jnp.einsum('bhtk,bhkv->bhtv', w_c, state)
        o = (jnp.einsum('bhtk,bhkv->bhtv', q_c * jnp.exp(cum_c)[..., None], state)
             + jnp.einsum('bhts,bhsv->bhtv', qk_c, u))
        new_state = (jnp.exp(cum_c[..., -1])[..., None, None] * state
                     + jnp.einsum('bhsk,bhsv->bhkv', k_c * jnp.exp(cum_c[..., -1:] - cum_c)[..., None], u))
        return new_state, o

    init = jnp.zeros((bsz, heads, dk, dv), v.dtype)
    _, o = lax.scan(step, init, (qc, kc, u_base, w_corr, qk, cum))
    return from_chunks(o)


def hgrn2_branch(q_raw, f_raw, i_raw, g_raw, lower_bound, norm_w):
    bsz, s, _ = q_raw.shape
    q = jax.nn.silu(q_raw)
    f = lower_bound + (1 - lower_bound) * jax.nn.sigmoid(f_raw)
    logf = jnp.log(jnp.maximum(f, F_MIN))
    k = (1 - lower_bound) * jax.nn.sigmoid(-f_raw)
    o = gla_chunked(q.reshape(bsz, s, HG_HEADS, HG_DK), k.reshape(bsz, s, HG_HEADS, HG_DK),
                    i_raw.reshape(bsz, s, HG_HEADS, HG_DV), logf.reshape(bsz, s, HG_HEADS, HG_DK))
    o = rms_norm(o, norm_w) * jax.nn.silu(g_raw.reshape(bsz, s, HG_HEADS, HG_DV))
    return o.reshape(bsz, s, MIX_W)


def mamba2_branch(z, xbc, dt_raw, conv_w, conv_b, dt_bias, a_log, d_skip, norm_w):
    bsz, s, _ = z.shape
    xbc = jax.nn.silu(causal_depthwise_conv(xbc, conv_w, conv_b))
    xs, bm, cm = jnp.split(xbc, [MIX_W, MIX_W + SSM_GROUPS * SSM_STATE], axis=-1)
    xs = xs.reshape(bsz, s, SSM_HEADS, SSM_HEADDIM)
    rep = SSM_HEADS // SSM_GROUPS
    bm = jnp.repeat(bm.reshape(bsz, s, SSM_GROUPS, SSM_STATE), rep, axis=2)
    cm = jnp.repeat(cm.reshape(bsz, s, SSM_GROUPS, SSM_STATE), rep, axis=2)
    dt = jax.nn.softplus(dt_raw + dt_bias)
    loga = -jnp.exp(a_log) * dt
    y = ssd_chunked(cm, bm, xs * dt[..., None], loga) + d_skip[:, None] * xs
    y = y.reshape(bsz, s, MIX_W) * jax.nn.silu(z)
    gw = MIX_W // SSM_GROUPS
    y = rms_norm(y.reshape(bsz, s, SSM_GROUPS, gw), norm_w.reshape(SSM_GROUPS, gw))
    return y.reshape(bsz, s, MIX_W)


def gdn_branch(qkv, z, b_raw, a_raw, conv_w, dt_bias, a_log, norm_w):
    bsz, s, _ = z.shape
    qkv = jax.nn.silu(causal_depthwise_conv(qkv, conv_w))
    q, k, v = jnp.split(qkv, [GDN_HEADS * GDN_DK, 2 * GDN_HEADS * GDN_DK], axis=-1)
    q = l2_normalize(q.reshape(bsz, s, GDN_HEADS, GDN_DK)) * (GDN_DK ** -0.5)
    k = l2_normalize(k.reshape(bsz, s, GDN_HEADS, GDN_DK))
    v = v.reshape(bsz, s, GDN_HEADS, GDN_DV)
    beta = jax.nn.sigmoid(b_raw)
    logg = -jnp.exp(a_log) * jax.nn.softplus(a_raw + dt_bias)
    o = gated_delta_chunked(q, k, v, logg, beta)
    o = rms_norm(o, norm_w) * jax.nn.silu(z.reshape(bsz, s, GDN_HEADS, GDN_DV))
    return o.reshape(bsz, s, MIX_W)


def setup_inputs(seed: int = 0) -> dict:
    key = jax.random.key(seed)
    ks = iter(jax.random.split(key, 40))

    def normal(shape, scale):
        return scale * jax.random.normal(next(ks), shape, jnp.float32)

    def uniform(shape, lo, hi):
        return jax.random.uniform(next(ks), shape, jnp.float32, lo, hi)

    def dt_bias_init(shape):
        dt = jnp.exp(uniform(shape, math.log(1e-3), math.log(1e-1)))
        return dt + jnp.log(-jnp.expm1(-dt))

    L = DEPTH
    return {
        'x': normal((BATCH, SEQ, D_MODEL), 1.0),
        'c': normal((BATCH, D_MODEL), 1.0),
        'w_ada': normal((L, D_MODEL, 6 * D_MODEL), 0.1 * D_MODEL ** -0.5),
        'b_ada': normal((L, 6 * D_MODEL), 0.01),
        'norm_mix': 1.0 + normal((L, D_MODEL), 0.02),
        'norm_ffn': 1.0 + normal((L, D_MODEL), 0.02),
        'w_in': normal((L, D_MODEL, IN_WIDTH), D_MODEL ** -0.5),
        'b_merge': normal((L, N_BRANCH * D_MODEL), 0.01),
        'hgrn_lb_logits': normal((L, HG_HEADS * HG_DK), 0.1),
        'hgrn_norm': 1.0 + normal((L, HG_DV), 0.02),
        'ssm_conv_w': normal((L, CONV_K, SSM_CONV_CH), CONV_K ** -0.5),
        'ssm_conv_b': normal((L, SSM_CONV_CH), 0.01),
        'ssm_dt_bias': dt_bias_init((L, SSM_HEADS)),
        'ssm_a_log': jnp.log(uniform((L, SSM_HEADS), 1.0, 16.0)),
        'ssm_d': 1.0 + normal((L, SSM_HEADS), 0.02),
        'ssm_norm': 1.0 + normal((L, MIX_W), 0.02),
        'gdn_conv_w': normal((L, CONV_K, GDN_QKV), CONV_K ** -0.5),
        'gdn_dt_bias': dt_bias_init((L, GDN_HEADS)),
        'gdn_a_log': jnp.log(uniform((L, GDN_HEADS), 1.0, 16.0)),
        'gdn_norm': 1.0 + normal((L, GDN_DV), 0.02),
        'w_branch': normal((L, N_BRANCH, MIX_W, D_MODEL), MIX_W ** -0.5),
        'w_out': normal((L, D_MODEL, D_MODEL), D_MODEL ** -0.5),
        'w_ffn_in': normal((L, D_MODEL, 2 * FFN_HIDDEN), D_MODEL ** -0.5),
        'w_ffn_out': normal((L, FFN_HIDDEN, D_MODEL), FFN_HIDDEN ** -0.5),
        'norm_final': 1.0 + normal((D_MODEL,), 0.02),
    }


def reference(x, c, w_ada, b_ada, norm_mix, norm_ffn, w_in, b_merge, hgrn_lb_logits, hgrn_norm,
              ssm_conv_w, ssm_conv_b, ssm_dt_bias, ssm_a_log, ssm_d, ssm_norm,
              gdn_conv_w, gdn_dt_bias, gdn_a_log, gdn_norm,
              w_branch, w_out, w_ffn_in, w_ffn_out, norm_final):
    bsz, s, _ = x.shape
    p = jax.nn.softmax(hgrn_lb_logits.astype(jnp.float32), axis=0)
    lower_bounds = (jnp.cumsum(p, axis=0) - p[0]).astype(x.dtype)
    c_act = jax.nn.silu(c)
    for layer in range(DEPTH):
        mod = c_act @ w_ada[layer] + b_ada[layer]
        sh1, sc1, g1, sh2, sc2, g2 = [m[:, None, :] for m in jnp.split(mod, 6, axis=-1)]
        h = rms_norm(x, norm_mix[layer]) * (1 + sc1) + sh1
        (hq, hf, hi, hg, sz, sxbc, sdt, gqkv, gz, gb, ga, gate_logits) = jnp.split(
            h @ w_in[layer], IN_SPLITS, axis=-1)
        y_hgrn = hgrn2_branch(hq, hf, hi, hg, lower_bounds[layer], hgrn_norm[layer])
        y_ssm = mamba2_branch(sz, sxbc, sdt, ssm_conv_w[layer], ssm_conv_b[layer], ssm_dt_bias[layer],
                              ssm_a_log[layer], ssm_d[layer], ssm_norm[layer])
        y_gdn = gdn_branch(gqkv, gz, gb, ga, gdn_conv_w[layer], gdn_dt_bias[layer], gdn_a_log[layer],
                           gdn_norm[layer])
        branch_out = jnp.einsum('nbsw,nwd->nbsd', jnp.stack([y_hgrn, y_ssm, y_gdn], axis=0), w_branch[layer])
        gates = jax.nn.sigmoid(gate_logits + b_merge[layer]).reshape(bsz, s, N_BRANCH, D_MODEL)
        merged = jnp.einsum('bsnd,nbsd->bsd', gates, branch_out)
        x = x + (1 + g1) * (merged @ w_out[layer])
        h = rms_norm(x, norm_ffn[layer]) * (1 + sc2) + sh2
        gate, up = jnp.split(h @ w_ffn_in[layer], 2, axis=-1)
        x = x + (1 + g2) * ((jax.nn.silu(gate) * up) @ w_ffn_out[layer])
    return rms_norm(x, norm_final)
```

```python
import functools

import numpy as np
import jax
import jax.numpy as jnp
from jax import lax
from jax.experimental import pallas as pl
from jax.experimental.pallas import tpu as pltpu

f32 = jnp.float32
_MXU = jnp.bfloat16

CHUNK = 64
NORM_EPS = 1e-6
F_MIN = 1e-30
N_BRANCH = 3
CONV_K = 4
HEAD = 128
N_HEADS = 6
MIX_W = HEAD * N_HEADS
SSM_P = 64
SSM_H = MIX_W // SSM_P
SSM_N = 128
SSM_G = 2
SSM_GW = MIX_W // SSM_G
SSM_XBC = MIX_W + 2 * SSM_G * SSM_N
GDN_QKV = 3 * MIX_W
LANE = 128
CONV_PAD = 8
GLA_LEVELS = (32, 16, 8, 4, 2, 1)
TOKENS_PER_STEP = 512
VMEM_LIMIT = 56 * 1024 * 1024


def _dot(a, b):
    return jnp.dot(a.astype(_MXU), b.astype(_MXU), preferred_element_type=f32)


def _dot_nt(a, b):
    return lax.dot_general(a.astype(_MXU), b.astype(_MXU), (((1,), (1,)), ((), ())),
                           preferred_element_type=f32)


def _dot_tn(a, b):
    return lax.dot_general(a.astype(_MXU), b.astype(_MXU), (((0,), (0,)), ((), ())),
                           preferred_element_type=f32)


def _pieces(x, n):
    out, r = [], x
    for i in range(n):
        p = r.astype(_MXU)
        out.append(p)
        if i + 1 < n:
            r = r - p.astype(f32)
    return out


def _sel_dot(sel, x, n=3):
    acc = None
    for p in _pieces(x, n):
        t = jnp.dot(sel, p, preferred_element_type=f32)
        acc = t if acc is None else acc + t
    return acc


def _dot_sel(x, sel, n=3):
    acc = None
    for p in _pieces(x, n):
        t = jnp.dot(p, sel, preferred_element_type=f32)
        acc = t if acc is None else acc + t
    return acc


def _dot_hp(a, b):
    ah, al = _pieces(a, 2)
    bh, bl = _pieces(b, 2)
    return (jnp.dot(ah, bh, preferred_element_type=f32)
            + jnp.dot(al, bh, preferred_element_type=f32)
            + jnp.dot(ah, bl, preferred_element_type=f32))


def _sigmoid(x):
    return jax.nn.sigmoid(x)


def _silu(x):
    return x * jax.nn.sigmoid(x)


def _softplus(x):
    return jnp.maximum(x, 0.0) + jnp.log1p(jnp.exp(-jnp.abs(x)))


def _rms(x):
    return x * lax.rsqrt(jnp.mean(x * x, axis=-1, keepdims=True) + NORM_EPS)


def _norm_mod(x, w, scale, shift):
    return (_rms(x) * w) * (1.0 + scale) + shift


def _iota2(shape):
    return (lax.broadcasted_iota(jnp.int32, shape, 0), lax.broadcasted_iota(jnp.int32, shape, 1))


def _conv_silu(xb_ref, xc_ref, cw_ref, bias, tt):
    for r0 in range(0, tt, CHUNK):
        acc = None
        for j in range(CONV_K):
            term = cw_ref[j:j + 1, :] * xb_ref[pl.ds(CONV_PAD - (CONV_K - 1) + j + r0, CHUNK), :]
            acc = term if acc is None else acc + term
        if bias is not None:
            acc = acc + bias
        xc_ref[pl.ds(r0, CHUNK), :] = _silu(acc)
    xb_ref[0:CONV_PAD, :] = xb_ref[tt:tt + CONV_PAD, :]


def _tri_inverse(a, rows, cols):
    eye = jnp.where(rows == cols, 1.0, 0.0)
    blk = (rows >> 4) == (cols >> 4)
    d = jnp.where(blk, a, 0.0)
    low = jnp.where(blk, 0.0, a)
    n = -d
    p = eye + n
    n2 = _dot_hp(n, n)
    p = p + _dot_hp(p, n2)
    n4 = _dot_hp(n2, n2)
    p = p + _dot_hp(p, n4)
    n8 = _dot_hp(n4, n4)
    p = p + _dot_hp(p, n8)
    m = _dot_hp(p, low)
    m2 = _dot_hp(m, m)
    m3 = _dot_hp(m, m2)
    return _dot_hp(eye - m + m2 - m3, p)


def _gla_select():
    c = CHUNK
    mats = [np.tril(np.ones((c, c), np.float32))]
    for w in GLA_LEVELS:
        m = np.zeros((c, c), np.float32)
        for t in range(c):
            mid = (t // (2 * w)) * 2 * w + w
            if t >= mid:
                m[t, mid:t + 1] = 1.0
            else:
                m[t, t + 1:mid] = 1.0
        mats.append(m)
    return np.concatenate(mats, axis=0)


def _expand_matrix(n_heads, width):
    m = np.zeros((LANE, n_heads * width), np.float32)
    for h in range(n_heads):
        m[h, h * width:(h + 1) * width] = 1.0
    return m


def _pad_lanes(v, width=LANE):
    return jnp.pad(v, [(0, 0)] * (v.ndim - 1) + [(0, width - v.shape[-1])])


def _ada_kernel(c_ref, w_ref, b_ref, o_ref):
    o_ref[...] = _dot(_silu(c_ref[...]), w_ref[...]) + b_ref[...]


def _ada(c, w_ada, b_ada):
    depth, d, n = w_ada.shape
    bsz = c.shape[0]
    tn = n // 4
    return pl.pallas_call(
        _ada_kernel,
        grid=(depth, n // tn),
        in_specs=[pl.BlockSpec((bsz, d), lambda l, j: (0, 0)),
                  pl.BlockSpec((None, d, tn), lambda l, j: (l, 0, j)),
                  pl.BlockSpec((None, 1, tn), lambda l, j: (l, 0, j))],
        out_specs=pl.BlockSpec((None, bsz, tn), lambda l, j: (l, 0, j)),
        out_shape=jax.ShapeDtypeStruct((depth, bsz, n), f32),
        compiler_params=pltpu.CompilerParams(dimension_semantics=("parallel", "parallel"),
                                             vmem_limit_bytes=VMEM_LIMIT),
        name="ada",
    )(c, w_ada, b_ada.reshape(depth, 1, n))


def _hgrn_kernel(x_ref, mod_ref, nw_ref, w_ref, lbl_ref, hn_ref, sel_ref, y_ref,
                 p_ref, q_ref, k_ref, eb_ref, st_ref, *, layer, tt):
    @pl.when(pl.program_id(1) == 0)
    def _():
        st_ref[...] = jnp.zeros_like(st_ref)

    hb = _norm_mod(x_ref[...], nw_ref[...], mod_ref[1:2, :], mod_ref[0:1, :]).astype(_MXU)
    for j in range(4):
        cs = slice(j * MIX_W, (j + 1) * MIX_W)
        p_ref[:, cs] = jnp.dot(hb, w_ref[:, cs], preferred_element_type=f32)

    lg = lbl_ref[...]
    e = jnp.exp(lg - jnp.max(lg, axis=0, keepdims=True))
    prob = e / jnp.sum(e, axis=0, keepdims=True)
    lb = prob[0:1, :]
    for i in range(1, layer + 1):
        lb = lb + prob[i:i + 1, :]
    lb = lb - prob[0:1, :]

    def chunk(c, carry):
        rows = pl.ds(pl.multiple_of(c * CHUNK, CHUNK), CHUNK)
        qraw = p_ref[rows, 0:MIX_W]
        fraw = p_ref[rows, MIX_W:2 * MIX_W]
        q_ref[...] = _silu(qraw)
        f = lb + (1.0 - lb) * _sigmoid(fraw)
        logf = jnp.log(jnp.maximum(f, F_MIN))
        k_ref[...] = (1.0 - lb) * _sigmoid(-fraw)
        eb_ref[...] = _sel_dot(sel_ref[...], logf)

        r, s = _iota2((CHUNK, CHUNK))
        for hd in range(N_HEADS):
            sl = slice(hd * HEAD, (hd + 1) * HEAD)
            qh = q_ref[:, sl]
            kh = k_ref[:, sl]
            vh = p_ref[rows, 2 * MIX_W + hd * HEAD:2 * MIX_W + (hd + 1) * HEAD]
            gh = p_ref[rows, 3 * MIX_W + hd * HEAD:3 * MIX_W + (hd + 1) * HEAD]
            b = eb_ref[0:CHUNK, sl]
            sc = jnp.where(r == s, _dot_nt(qh, kh), 0.0)
            for i, w in enumerate(GLA_LEVELS):
                ex = jnp.exp(eb_ref[(i + 1) * CHUNK:(i + 2) * CHUNK, sl])
                sh = w.bit_length()
                pair = ((r >> sh) == (s >> sh)) & ((r & w) != 0) & ((s & w) == 0)
                sc = sc + jnp.where(pair, _dot_nt(qh * ex, kh * ex), 0.0)
            st = st_ref[hd]
            o = _dot(sc, vh) + _dot_nt(qh * jnp.exp(b), st)
            bend = b[CHUNK - 1:CHUNK, :]
            st_ref[hd] = st * jnp.exp(bend) + _dot_tn(vh, kh * jnp.exp(bend - b))
            y = (_rms(o) * hn_ref[...]) * _silu(gh)
            y_ref[rows, sl] = y.astype(y_ref.dtype)
        return carry

    lax.fori_loop(0, tt // CHUNK, chunk, 0)


def _hgrn(x, mod, norm_w, w, lb_logits, head_norm, *, layer, tt):
    bsz, seq, d = x.shape
    depth = lb_logits.shape[0]
    sel = jnp.asarray(_gla_select(), _MXU)
    const = lambda shape: pl.BlockSpec(shape, lambda b, s: (0,) * len(shape), pipeline_mode=pl.Buffered(1))
    return pl.pallas_call(
        functools.partial(_hgrn_kernel, layer=layer, tt=tt),
        grid=(bsz, seq // tt),
        in_specs=[pl.BlockSpec((None, tt, d), lambda b, s: (b, s, 0)),
                  pl.BlockSpec((None, 6, d), lambda b, s: (b, 0, 0)),
                  const((1, d)),
                  const((d, 4 * MIX_W)),
                  const((depth, MIX_W)),
                  const((1, HEAD)),
                  const(sel.shape)],
        out_specs=pl.BlockSpec((None, tt, MIX_W), lambda b, s: (b, s, 0)),
        out_shape=jax.ShapeDtypeStruct((bsz, seq, MIX_W), _MXU),
        scratch_shapes=[pltpu.VMEM((tt, 4 * MIX_W), f32),
                        pltpu.VMEM((CHUNK, MIX_W), f32),
                        pltpu.VMEM((CHUNK, MIX_W), f32),
                        pltpu.VMEM(((len(GLA_LEVELS) + 1) * CHUNK, MIX_W), f32),
                        pltpu.VMEM((N_HEADS, HEAD, HEAD), f32)],
        compiler_params=pltpu.CompilerParams(dimension_semantics=("parallel", "arbitrary"),
                                             vmem_limit_bytes=VMEM_LIMIT),
        name="hgrn",
    )(x, mod, norm_w, w, lb_logits, head_norm, sel)


def _ssd_kernel(x_ref, mod_ref, nw_ref, w_ref, cw_ref, cb_ref, dtb_ref, alog_ref, dsk_ref, sn_ref,
                tril_ref, exp_ref, y_ref, xb_ref, xc_ref, pz_ref, ex_ref, yh_ref, st_ref, *, tt):
    @pl.when(pl.program_id(1) == 0)
    def _():
        st_ref[...] = jnp.zeros_like(st_ref)
        xb_ref[0:CONV_PAD, :] = jnp.zeros((CONV_PAD, SSM_XBC), f32)

    hb = _norm_mod(x_ref[...], nw_ref[...], mod_ref[1:2, :], mod_ref[0:1, :]).astype(_MXU)
    xb_ref[CONV_PAD:CONV_PAD + tt, :] = jnp.dot(hb, w_ref[:, 0:SSM_XBC], preferred_element_type=f32)
    pz_ref[...] = jnp.dot(hb, w_ref[:, SSM_XBC:], preferred_element_type=f32)
    _conv_silu(xb_ref, xc_ref, cw_ref, cb_ref[...], tt)
    neg_a = -jnp.exp(alog_ref[...])

    def chunk(c, carry):
        rows = pl.ds(pl.multiple_of(c * CHUNK, CHUNK), CHUNK)
        xs = xc_ref[rows, 0:MIX_W]
        dt = _softplus(pz_ref[rows, MIX_W:MIX_W + LANE] + dtb_ref[...])
        cum = _sel_dot(tril_ref[...], neg_a * dt)
        cum_t = cum.T
        cend = cum[CHUNK - 1:CHUNK, :]
        ex_ref[0:CHUNK, :] = _dot_sel(dt, exp_ref[...])
        ex_ref[CHUNK:2 * CHUNK, :] = _dot_sel(jnp.exp(cum), exp_ref[...])
        ex_ref[2 * CHUNK:3 * CHUNK, :] = _dot_sel(jnp.exp(cend - cum), exp_ref[...])
        xdt = xs * ex_ref[0:CHUNK, :]
        xend = xdt * ex_ref[2 * CHUNK:3 * CHUNK, :]
        r, s = _iota2((CHUNK, CHUNK))
        causal = r >= s
        for g in range(SSM_G):
            gs = slice(g * SSM_GW, (g + 1) * SSM_GW)
            bm = xc_ref[rows, MIX_W + g * SSM_N:MIX_W + (g + 1) * SSM_N]
            cm = xc_ref[rows, MIX_W + (SSM_G + g) * SSM_N:MIX_W + (SSM_G + g + 1) * SSM_N]
            cb = _dot_nt(cm, bm)
            st = st_ref[g]
            y_inter = _dot(cm, st) * ex_ref[CHUNK:2 * CHUNK, gs]
            for hh in range(SSM_H // SSM_G):
                h = g * (SSM_H // SSM_G) + hh
                diff = jnp.where(causal, cum[:, h:h + 1] - cum_t[h:h + 1, :], 0.0)
                seg = jnp.where(causal, jnp.exp(diff), 0.0)
                hs = slice(h * SSM_P, (h + 1) * SSM_P)
                yh_ref[:, hs] = _dot(cb * seg, xdt[:, hs])
            st_ref[g] = st * ex_ref[2 * CHUNK - 1:2 * CHUNK, gs] + _dot_tn(bm, xend[:, gs])
            yg = (yh_ref[:, gs] + y_inter + dsk_ref[:, gs] * xs[:, gs]) * _silu(pz_ref[rows, gs])
            y_ref[rows, gs] = (_rms(yg) * sn_ref[:, gs]).astype(y_ref.dtype)
        return carry

    lax.fori_loop(0, tt // CHUNK, chunk, 0)


def _ssd(x, mod, norm_w, w, conv_w, conv_b, dt_bias, a_log, d_skip, ssm_norm, *, tt):
    bsz, seq, d = x.shape
    tril = jnp.asarray(np.tril(np.ones((CHUNK, CHUNK), np.float32)), _MXU)
    expand = jnp.asarray(_expand_matrix(SSM_H, SSM_P), _MXU)
    const = lambda shape: pl.BlockSpec(shape, lambda b, s: (0,) * len(shape), pipeline_mode=pl.Buffered(1))
    nz = w.shape[1] - SSM_XBC
    return pl.pallas_call(
        functools.partial(_ssd_kernel, tt=tt),
        grid=(bsz, seq // tt),
        in_specs=[pl.BlockSpec((None, tt, d), lambda b, s: (b, s, 0)),
                  pl.BlockSpec((None, 6, d), lambda b, s: (b, 0, 0)),
                  const((1, d)),
                  const(w.shape),
                  const((CONV_K, SSM_XBC)),
                  const((1, SSM_XBC)),
                  const((1, LANE)),
                  const((1, LANE)),
                  const((1, MIX_W)),
                  const((1, MIX_W)),
                  const(tril.shape),
                  const(expand.shape)],
        out_specs=pl.BlockSpec((None, tt, MIX_W), lambda b, s: (b, s, 0)),
        out_shape=jax.ShapeDtypeStruct((bsz, seq, MIX_W), _MXU),
        scratch_shapes=[pltpu.VMEM((tt + CONV_PAD, SSM_XBC), f32),
                        pltpu.VMEM((tt, SSM_XBC), f32),
                        pltpu.VMEM((tt, nz), f32),
                        pltpu.VMEM((3 * CHUNK, MIX_W), f32),
                        pltpu.VMEM((CHUNK, MIX_W), f32),
                        pltpu.VMEM((SSM_G, SSM_N, SSM_GW), f32)],
        compiler_params=pltpu.CompilerParams(dimension_semantics=("parallel", "arbitrary"),
                                             vmem_limit_bytes=VMEM_LIMIT),
        name="ssd",
    )(x, mod, norm_w, w, conv_w, conv_b, dt_bias, a_log, d_skip, ssm_norm, tril, expand)


def _gdn_kernel(x_ref, mod_ref, nw_ref, w_ref, cw_ref, dtb_ref, alog_ref, gn_ref, tril_ref, exp_ref, y_ref,
                xb_ref, xc_ref, pz_ref, ex_ref, st_ref, *, tt):
    @pl.when(pl.program_id(1) == 0)
    def _():
        st_ref[...] = jnp.zeros_like(st_ref)
        xb_ref[0:CONV_PAD, :] = jnp.zeros((CONV_PAD, GDN_QKV), f32)

    hb = _norm_mod(x_ref[...], nw_ref[...], mod_ref[1:2, :], mod_ref[0:1, :]).astype(_MXU)
    for j in range(3):
        cs = slice(j * MIX_W, (j + 1) * MIX_W)
        xb_ref[CONV_PAD:CONV_PAD + tt, cs] = jnp.dot(hb, w_ref[:, cs], preferred_element_type=f32)
    pz_ref[...] = jnp.dot(hb, w_ref[:, GDN_QKV:], preferred_element_type=f32)
    _conv_silu(xb_ref, xc_ref, cw_ref, None, tt)
    neg_a = -jnp.exp(alog_ref[...])

    def chunk(c, carry):
        rows = pl.ds(pl.multiple_of(c * CHUNK, CHUNK), CHUNK)
        beta = _sigmoid(pz_ref[rows, MIX_W:MIX_W + LANE])
        logg = neg_a * _softplus(pz_ref[rows, MIX_W + LANE:MIX_W + 2 * LANE] + dtb_ref[...])
        cum = _sel_dot(tril_ref[...], logg)
        cum_t = cum.T
        cend = cum[CHUNK - 1:CHUNK, :]
        ex_ref[0:CHUNK, :] = _dot_sel(beta, exp_ref[...])
        ex_ref[CHUNK:2 * CHUNK, :] = _dot_sel(jnp.exp(cum), exp_ref[...])
        ex_ref[2 * CHUNK:3 * CHUNK, :] = _dot_sel(jnp.exp(cend - cum), exp_ref[...])
        r, s = _iota2((CHUNK, CHUNK))
        causal = r >= s
        for hd in range(N_HEADS):
            sl = slice(hd * HEAD, (hd + 1) * HEAD)
            qh = xc_ref[rows, hd * HEAD:(hd + 1) * HEAD]
            kh = xc_ref[rows, MIX_W + hd * HEAD:MIX_W + (hd + 1) * HEAD]
            vh = xc_ref[rows, 2 * MIX_W + hd * HEAD:2 * MIX_W + (hd + 1) * HEAD]
            qh = qh * lax.rsqrt(jnp.sum(qh * qh, axis=-1, keepdims=True) + NORM_EPS) * (HEAD ** -0.5)
            kh = kh * lax.rsqrt(jnp.sum(kh * kh, axis=-1, keepdims=True) + NORM_EPS)
            bx = ex_ref[0:CHUNK, sl]
            ecum = ex_ref[CHUNK:2 * CHUNK, sl]
            eend = ex_ref[2 * CHUNK:3 * CHUNK, sl]
            diff = jnp.where(causal, cum[:, hd:hd + 1] - cum_t[hd:hd + 1, :], 0.0)
            decay = jnp.where(causal, jnp.exp(diff), 0.0)
            a_low = jnp.where(r > s, beta[:, hd:hd + 1] * _dot_nt(kh, kh) * decay, 0.0)
            t_inv = _tri_inverse(a_low, r, s)
            u_base = _dot(t_inv, vh * bx)
            w_corr = _dot(t_inv, kh * (bx * ecum))
            qk = _dot_nt(qh, kh) * decay
            st = st_ref[hd]
            u = u_base - _dot(w_corr, st)
            o = _dot(qh * ecum, st) + _dot(qk, u)
            st_ref[hd] = st * ecum[CHUNK - 1:CHUNK, :] + _dot_tn(kh * eend, u)
            y = (_rms(o) * gn_ref[...]) * _silu(pz_ref[rows, sl])
            y_ref[rows, sl] = y.astype(y_ref.dtype)
        return carry

    lax.fori_loop(0, tt // CHUNK, chunk, 0)


def _gdn(x, mod, norm_w, w, conv_w, dt_bias, a_log, head_norm, *, tt):
    bsz, seq, d = x.shape
    tril = jnp.asarray(np.tril(np.ones((CHUNK, CHUNK), np.float32)), _MXU)
    expand = jnp.asarray(_expand_matrix(N_HEADS, HEAD), _MXU)
    const = lambda shape: pl.BlockSpec(shape, lambda b, s: (0,) * len(shape), pipeline_mode=pl.Buffered(1))
    nz = w.shape[1] - GDN_QKV
    return pl.pallas_call(
        functools.partial(_gdn_kernel, tt=tt),
        grid=(bsz, seq // tt),
        in_specs=[pl.BlockSpec((None, tt, d), lambda b, s: (b, s, 0)),
                  pl.BlockSpec((None, 6, d), lambda b, s: (b, 0, 0)),
                  const((1, d)),
                  const(w.shape),
                  const((CONV_K, GDN_QKV)),
                  const((1, LANE)),
                  const((1, LANE)),
                  const((1, HEAD)),
                  const(tril.shape),
                  const(expand.shape)],
        out_specs=pl.BlockSpec((None, tt, MIX_W), lambda b, s: (b, s, 0)),
        out_shape=jax.ShapeDtypeStruct((bsz, seq, MIX_W), _MXU),
        scratch_shapes=[pltpu.VMEM((tt + CONV_PAD, GDN_QKV), f32),
                        pltpu.VMEM((tt, GDN_QKV), f32),
                        pltpu.VMEM((tt, nz), f32),
                        pltpu.VMEM((3 * CHUNK, MIX_W), f32),
                        pltpu.VMEM((N_HEADS, HEAD, HEAD), f32)],
        compiler_params=pltpu.CompilerParams(dimension_semantics=("parallel", "arbitrary"),
                                             vmem_limit_bytes=VMEM_LIMIT),
        name="gdn",
    )(x, mod, norm_w, w, conv_w, dt_bias, a_log, head_norm, tril, expand)


def _merge_kernel(x_ref, mod_ref, nw_ref, wg_ref, bm_ref, y0_ref, y1_ref, y2_ref, wb_ref, wo_ref, o_ref):
    x = x_ref[...]
    d = x.shape[-1]
    hb = _norm_mod(x, nw_ref[...], mod_ref[1:2, :], mod_ref[0:1, :]).astype(_MXU)
    merged = None
    for n, y_ref in enumerate((y0_ref, y1_ref, y2_ref)):
        cs = slice(n * d, (n + 1) * d)
        gate = _sigmoid(jnp.dot(hb, wg_ref[:, cs], preferred_element_type=f32) + bm_ref[:, cs])
        term = gate * jnp.dot(y_ref[...], wb_ref[n], preferred_element_type=f32)
        merged = term if merged is None else merged + term
    o_ref[...] = x + (1.0 + mod_ref[2:3, :]) * _dot(merged, wo_ref[...])


def _merge(x, mod, norm_w, w_gate, b_merge, ys, w_branch, w_out, *, tm):
    bsz, seq, d = x.shape
    const = lambda shape: pl.BlockSpec(shape, lambda b, s: (0,) * len(shape), pipeline_mode=pl.Buffered(1))
    yspec = pl.BlockSpec((None, tm, MIX_W), lambda b, s: (b, s, 0))
    return pl.pallas_call(
        _merge_kernel,
        grid=(bsz, seq // tm),
        in_specs=[pl.BlockSpec((None, tm, d), lambda b, s: (b, s, 0)),
                  pl.BlockSpec((None, 6, d), lambda b, s: (b, 0, 0)),
                  const((1, d)),
                  const(w_gate.shape),
                  const((1, N_BRANCH * d)),
                  yspec, yspec, yspec,
                  const(w_branch.shape),
                  const(w_out.shape)],
        out_specs=pl.BlockSpec((None, tm, d), lambda b, s: (b, s, 0)),
        out_shape=jax.ShapeDtypeStruct((bsz, seq, d), f32),
        compiler_params=pltpu.CompilerParams(dimension_semantics=("parallel", "parallel"),
                                             vmem_limit_bytes=VMEM_LIMIT),
        name="merge",
    )(x, mod, norm_w, w_gate, b_merge, *ys, w_branch, w_out)


def _ffn_kernel(x_ref, mod_ref, nw_ref, wi_ref, wo_ref, fw_ref, o_ref, *, hidden, n_split, final):
    x = x_ref[...]
    hb = _norm_mod(x, nw_ref[...], mod_ref[4:5, :], mod_ref[3:4, :]).astype(_MXU)
    step = hidden // n_split
    acc = None
    for j in range(n_split):
        gate = jnp.dot(hb, wi_ref[:, j * step:(j + 1) * step], preferred_element_type=f32)
        up = jnp.dot(hb, wi_ref[:, hidden + j * step:hidden + (j + 1) * step], preferred_element_type=f32)
        t = _dot(_silu(gate) * up, wo_ref[j * step:(j + 1) * step, :])
        acc = t if acc is None else acc + t
    out = x + (1.0 + mod_ref[5:6, :]) * acc
    if final:
        out = _rms(out) * fw_ref[...]
    o_ref[...] = out


def _ffn(x, mod, norm_w, w_in, w_out, final_w, *, tm, final):
    bsz, seq, d = x.shape
    hidden = w_out.shape[0]
    const = lambda shape: pl.BlockSpec(shape, lambda b, s: (0,) * len(shape), pipeline_mode=pl.Buffered(1))
    return pl.pallas_call(
        functools.partial(_ffn_kernel, hidden=hidden, n_split=2, final=final),
        grid=(bsz, seq // tm),
        in_specs=[pl.BlockSpec((None, tm, d), lambda b, s: (b, s, 0)),
                  pl.BlockSpec((None, 6, d), lambda b, s: (b, 0, 0)),
                  const((1, d)),
                  const(w_in.shape),
                  const(w_out.shape),
                  const((1, d))],
        out_specs=pl.BlockSpec((None, tm, d), lambda b, s: (b, s, 0)),
        out_shape=jax.ShapeDtypeStruct((bsz, seq, d), f32),
        compiler_params=pltpu.CompilerParams(dimension_semantics=("parallel", "parallel"),
                                             vmem_limit_bytes=VMEM_LIMIT),
        name="ffn",
    )(x, mod, norm_w, w_in, w_out, final_w)


def _split_w_in(w):
    sizes = (MIX_W, MIX_W, MIX_W, MIX_W, MIX_W, SSM_XBC, SSM_H, GDN_QKV, MIX_W, N_HEADS, N_HEADS)
    offs = np.concatenate([[0], np.cumsum(sizes)])
    col = lambda i: w[:, int(offs[i]):int(offs[i + 1])]
    w_hgrn = w[:, 0:4 * MIX_W]
    w_ssd = jnp.concatenate([col(5), col(4), _pad_lanes(col(6))], axis=1)
    w_gdn = jnp.concatenate([col(7), col(8), _pad_lanes(col(9)), _pad_lanes(col(10))], axis=1)
    w_gate = w[:, int(offs[-1]):]
    return tuple(m.astype(_MXU) for m in (w_hgrn, w_ssd, w_gdn, w_gate))


def kernel(x, c, w_ada, b_ada, norm_mix, norm_ffn, w_in, b_merge, hgrn_lb_logits, hgrn_norm, ssm_conv_w, ssm_conv_b, ssm_dt_bias, ssm_a_log, ssm_d, ssm_norm, gdn_conv_w, gdn_dt_bias, gdn_a_log, gdn_norm, w_branch, w_out, w_ffn_in, w_ffn_out, norm_final):
    bsz, seq, d = x.shape
    depth = w_in.shape[0]
    tt = min(seq, TOKENS_PER_STEP)
    mod_all = _ada(c, w_ada, b_ada).reshape(depth, bsz, 6, d)
    for layer in range(depth):
        mod = mod_all[layer]
        w_hgrn, w_ssd, w_gdn, w_gate = _split_w_in(w_in[layer])
        nmix = norm_mix[layer].reshape(1, d)
        y_hgrn = _hgrn(x, mod, nmix, w_hgrn, hgrn_lb_logits, hgrn_norm[layer].reshape(1, HEAD),
                       layer=layer, tt=tt)
        y_ssd = _ssd(x, mod, nmix, w_ssd, ssm_conv_w[layer], ssm_conv_b[layer].reshape(1, SSM_XBC),
                     _pad_lanes(ssm_dt_bias[layer].reshape(1, SSM_H)), _pad_lanes(ssm_a_log[layer].reshape(1, SSM_H)),
                     jnp.repeat(ssm_d[layer], SSM_P).reshape(1, MIX_W), ssm_norm[layer].reshape(1, MIX_W), tt=tt)
        y_gdn = _gdn(x, mod, nmix, w_gdn, gdn_conv_w[layer],
                     _pad_lanes(gdn_dt_bias[layer].reshape(1, N_HEADS)), _pad_lanes(gdn_a_log[layer].reshape(1, N_HEADS)),
                     gdn_norm[layer].reshape(1, HEAD), tt=tt)
        x = _merge(x, mod, nmix, w_gate, b_merge[layer].reshape(1, N_BRANCH * d), (y_hgrn, y_ssd, y_gdn),
                   w_branch[layer].astype(_MXU), w_out[layer].astype(_MXU), tm=tt)
        x = _ffn(x, mod, norm_ffn[layer].reshape(1, d), w_ffn_in[layer].astype(_MXU), w_ffn_out[layer].astype(_MXU),
                 norm_final.reshape(1, d), tm=tt, final=(layer == depth - 1))
    return x
```
